```python
import jax, jax.numpy as jnp
from jax import lax
import numpy as np

D_MODEL = 1024
BATCH = 4
SEQ = 8192
DEPTH = 1

CONV_WIDTH = 512
CONV_K = 3
ATTN_GROUPS = ((128, 1), (512, 4), (2048, 16))
N_GROUPS = 3
HEADS_PER_GROUP = 4
N_HEADS = N_GROUPS * HEADS_PER_GROUP
HEAD_DIM = 64
ATTN_WIDTH = N_HEADS * HEAD_DIM
ATTN_BLOCK = 128
D_FF = 2816
FFN_K = 3
EPS = 1e-6
NEG_INF = -1e30
IN_SIZES = (CONV_WIDTH, CONV_WIDTH, CONV_WIDTH, ATTN_WIDTH, ATTN_WIDTH, ATTN_WIDTH, D_MODEL, D_MODEL)
D_IN = sum(IN_SIZES)

kernel_name = "hybrid_shortconv_dilated_swa_convffn"


def rms_norm(x, g):
    xf = x.astype(jnp.float32)
    y = xf * lax.rsqrt(jnp.mean(xf * xf, axis=-1, keepdims=True) + EPS)
    return (y * g.astype(jnp.float32)).astype(x.dtype)


def causal_dwconv(x, w, b):
    K, C = w.shape
    y = lax.conv_general_dilated(
        x, w[:, None, :], window_strides=(1,), padding=[(K - 1, 0)],
        dimension_numbers=("NWC", "WIO", "NWC"), feature_group_count=C)
    return y + b


def dilated_window_attention(q, k, v, window, dilation):
    B, S, H, dh = q.shape
    Q = ATTN_BLOCK
    n_back = window // dilation
    L = S // dilation
    Lp = L + (-L) % Q
    nb = Lp // Q

    def to_streams(t):
        t = t.reshape(B, L, dilation, H, dh).transpose(0, 2, 1, 3, 4)
        return jnp.pad(t, ((0, 0), (0, 0), (0, Lp - L), (0, 0), (0, 0)))

    qs, ks, vs = to_streams(q), to_streams(k), to_streams(v)
    qb = qs.reshape(B, dilation, nb, Q, H, dh)

    def band_blocks(t):
        tp = jnp.pad(t, ((0, 0), (0, 0), (Q, 0), (0, 0), (0, 0)))
        prev = tp[:, :, :Lp].reshape(B, dilation, nb, Q, H, dh)
        cur = t.reshape(B, dilation, nb, Q, H, dh)
        return jnp.concatenate([prev, cur], axis=3)

    kb, vb = band_blocks(ks), band_blocks(vs)
    scores = jnp.einsum("brnqhe,brnkhe->brnhqk", qb, kb,
                        preferred_element_type=jnp.float32) * (dh ** -0.5)
    qi = jnp.arange(Q)[:, None]
    kj = jnp.arange(2 * Q)[None, :]
    dist = qi + Q - kj
    key_pos = jnp.arange(nb)[:, None, None] * Q + kj - Q
    valid = (dist >= 0) & (dist <= n_back) & (key_pos >= 0)
    scores = jnp.where(valid[:, None], scores, NEG_INF)
    m = jnp.max(scores, axis=-1)
    p = jnp.exp(scores - m[..., None])
    l = jnp.sum(p, axis=-1)
    o = jnp.einsum("brnhqk,brnkhe->brnqhe", p.astype(v.dtype), vb,
                   preferred_element_type=jnp.float32)
    o = o / jnp.swapaxes(l, 3, 4)[..., None]

    def from_streams(t):
        t = t.reshape((B, dilation, Lp) + t.shape[4:])[:, :, :L]
        t = jnp.moveaxis(t, 1, 2)
        return t.reshape((B, S) + t.shape[3:])

    return (from_streams(o), from_streams(jnp.swapaxes(m, 3, 4)),
            from_streams(jnp.swapaxes(l, 3, 4)))


def setup_inputs(seed: int = 0) -> dict:
    key = jax.random.key(seed)
    ks = jax.random.split(key, 16)
    f32 = jnp.float32

    def nrm(k, shape, scale):
        return jax.random.normal(k, shape, f32) * scale

    return {
        "x": nrm(ks[0], (BATCH, SEQ, D_MODEL), 1.0),
        "norm_mix_g": 1.0 + nrm(ks[1], (DEPTH, D_MODEL), 0.02),
        "w_in": nrm(ks[2], (DEPTH, D_MODEL, D_IN), D_MODEL ** -0.5),
        "b_gate": nrm(ks[3], (DEPTH, 2, D_MODEL), 0.01),
        "conv_a_w": nrm(ks[4], (DEPTH, CONV_K, CONV_WIDTH), CONV_K ** -0.5),
        "conv_a_b": nrm(ks[5], (DEPTH, CONV_WIDTH), 0.01),
        "w_proj_a": nrm(ks[6], (DEPTH, CONV_WIDTH, D_MODEL), CONV_WIDTH ** -0.5),
        "w_proj_b": nrm(ks[7], (DEPTH, ATTN_WIDTH, D_MODEL), ATTN_WIDTH ** -0.5),
        "w_out": nrm(ks[8], (DEPTH, D_MODEL, D_MODEL), D_MODEL ** -0.5),
        "norm_ffn_g": 1.0 + nrm(ks[9], (DEPTH, D_MODEL), 0.02),
        "w_up": nrm(ks[10], (DEPTH, D_MODEL, 2 * D_FF), D_MODEL ** -0.5),
        "ffn_conv_w": nrm(ks[11], (DEPTH, FFN_K, 2 * D_FF), FFN_K ** -0.5),
        "ffn_conv_b": nrm(ks[12], (DEPTH, 2 * D_FF), 0.01),
        "w_down": nrm(ks[13], (DEPTH, D_FF, D_MODEL), D_FF ** -0.5),
        "final_norm_g": 1.0 + nrm(ks[14], (D_MODEL,), 0.02),
    }


def reference(x, norm_mix_g, w_in, b_gate, conv_a_w, conv_a_b, w_proj_a, w_proj_b,
              w_out, norm_ffn_g, w_up, ffn_conv_w, ffn_conv_b, w_down, final_norm_g):
    B, S, _ = x.shape
    split_points = [int(s) for s in np.cumsum(IN_SIZES)[:-1]]
    for layer in range(DEPTH):
        h = rms_norm(x, norm_mix_g[layer])
        proj = h @ w_in[layer]
        a_b, a_c, a_v, q, k, v, g_a, g_b = jnp.split(proj, split_points, axis=-1)

        y_a = a_b * causal_dwconv(a_c * a_v, conv_a_w[layer], conv_a_b[layer])
        y_a = y_a @ w_proj_a[layer]

        q = q.reshape(B, S, N_GROUPS, HEADS_PER_GROUP, HEAD_DIM)
        k = k.reshape(B, S, N_GROUPS, HEADS_PER_GROUP, HEAD_DIM)
        v = v.reshape(B, S, N_GROUPS, HEADS_PER_GROUP, HEAD_DIM)
        outs, ms, ls = [], [], []
        for gi, (window, dilation) in enumerate(ATTN_GROUPS):
            o_g, m_g, l_g = dilated_window_attention(
                q[:, :, gi], k[:, :, gi], v[:, :, gi], window, dilation)
            outs.append(o_g)
            ms.append(m_g)
            ls.append(l_g)
        m_all = jnp.stack(ms, axis=2)
        l_all = jnp.stack(ls, axis=2)
        o_all = jnp.stack(outs, axis=2)
        w_den = l_all * jnp.exp(m_all - jnp.max(m_all, axis=2, keepdims=True))
        alpha = w_den / jnp.sum(w_den, axis=2, keepdims=True)
        y_b = (alpha[..., None] * o_all).reshape(B, S, ATTN_WIDTH).astype(x.dtype)
        y_b = y_b @ w_proj_b[layer]

        merged = (jax.nn.sigmoid(g_a + b_gate[layer, 0]) * y_a
                  + jax.nn.sigmoid(g_b + b_gate[layer, 1]) * y_b)
        x = x + merged @ w_out[layer]

        h = rms_norm(x, norm_ffn_g[layer])
        up = causal_dwconv(h @ w_up[layer], ffn_conv_w[layer], ffn_conv_b[layer])
        gate, val = jnp.split(up, 2, axis=-1)
        x = x + (jax.nn.silu(gate) * val) @ w_down[layer]
    return rms_norm(x, final_norm_g)
```

```python
import functools

import jax
import jax.numpy as jnp
from jax import lax
from jax.experimental import pallas as pl
from jax.experimental.pallas import tpu as pltpu

D_MODEL = 1024
CONV_WIDTH = 512
CONV_K = 3
ATTN_GROUPS = ((128, 1), (512, 4), (2048, 16))
N_GROUPS = 3
HEADS_PER_GROUP = 4
HEAD_DIM = 64
GROUP_WIDTH = HEADS_PER_GROUP * HEAD_DIM
ATTN_WIDTH = N_GROUPS * GROUP_WIDTH
ATTN_BLOCK = 128
D_FF = 2816
FFN_K = 3
EPS = 1e-6
NEG_INF = -1e30

OFF_AB, OFF_AC, OFF_AV = 0, CONV_WIDTH, 2 * CONV_WIDTH
OFF_QKV = 3 * CONV_WIDTH
OFF_GA = OFF_QKV + 3 * ATTN_WIDTH
OFF_GB = OFF_GA + D_MODEL
D_IN = OFF_GB + D_MODEL

CARRY_ROWS = 8
VMEM_LIMIT_BYTES = 56 * 1024 * 1024

BF16 = jnp.bfloat16
F32 = jnp.float32


def _const_spec(shape):
    return pl.BlockSpec(shape, lambda *_: (0,) * len(shape), pipeline_mode=pl.Buffered(1))


def _rms_norm(x, g):
    return x * lax.rsqrt(jnp.mean(x * x, axis=-1, keepdims=True) + EPS) * g


def _causal_conv3(scr, cur, w_ref, b_ref, cols, tm):
    c0 = CARRY_ROWS
    return (w_ref[0:1, cols] * scr[c0 - 2:c0 - 2 + tm, cols]
            + w_ref[1:2, cols] * scr[c0 - 1:c0 - 1 + tm, cols]
            + w_ref[2:3, cols] * cur + b_ref[0:1, cols])


def _mixer_in_kernel(x_ref, g_ref, win_ref, bg_ref, cw_ref, cb_ref, wa_ref,
                     q_ref, k_ref, v_ref, ga_ref, sb_ref, u_scr, *, tm):
    t = pl.program_id(1)
    h = _rms_norm(x_ref[0], g_ref[...]).astype(BF16)

    @pl.when(t == 0)
    def _():
        u_scr[0:CARRY_ROWS, :] = jnp.zeros((CARRY_ROWS, CONV_WIDTH), F32)

    p1 = jnp.dot(h, win_ref[:, OFF_AB:OFF_QKV], preferred_element_type=F32)
    u = p1[:, OFF_AC:OFF_AV] * p1[:, OFF_AV:OFF_QKV]
    u_scr[CARRY_ROWS:CARRY_ROWS + tm, :] = u
    conv = _causal_conv3(u_scr, u, cw_ref, cb_ref, slice(0, CONV_WIDTH), tm)
    u_scr[0:CARRY_ROWS, :] = u_scr[tm:tm + CARRY_ROWS, :]
    ya = (p1[:, OFF_AB:OFF_AC] * conv).astype(BF16)
    pa = jnp.dot(ya, wa_ref[...], preferred_element_type=F32)

    g_a = jnp.dot(h, win_ref[:, OFF_GA:OFF_GB], preferred_element_type=F32)
    ga_ref[0] = (jax.nn.sigmoid(g_a + bg_ref[0:1, :]) * pa).astype(BF16)
    g_b = jnp.dot(h, win_ref[:, OFF_GB:D_IN], preferred_element_type=F32)
    sb_ref[0] = jax.nn.sigmoid(g_b + bg_ref[1:2, :]).astype(BF16)

    qkv = jnp.dot(h, win_ref[:, OFF_QKV:OFF_GA], preferred_element_type=F32)
    q_ref[0] = (qkv[:, 0:ATTN_WIDTH] * (HEAD_DIM ** -0.5)).astype(BF16)
    k_ref[0] = qkv[:, ATTN_WIDTH:2 * ATTN_WIDTH].astype(BF16)
    v_ref[0] = qkv[:, 2 * ATTN_WIDTH:3 * ATTN_WIDTH].astype(BF16)


def _mixer_in(x, g, win, bg, cw, cb, wa, *, tm):
    B, S, D = x.shape
    row = lambda w: pl.BlockSpec((1, tm, w), lambda b, t: (b, t, 0))
    out_shapes = [jax.ShapeDtypeStruct((B, S, ATTN_WIDTH), BF16)] * 3 + [jax.ShapeDtypeStruct((B, S, D), BF16)] * 2
    return pl.pallas_call(
        functools.partial(_mixer_in_kernel, tm=tm),
        grid=(B, S // tm),
        in_specs=[row(D), _const_spec((1, D)), _const_spec((D, D_IN)), _const_spec((2, D)),
                  _const_spec((CONV_K, CONV_WIDTH)), _const_spec((1, CONV_WIDTH)), _const_spec((CONV_WIDTH, D))],
        out_specs=[row(ATTN_WIDTH)] * 3 + [row(D)] * 2,
        out_shape=out_shapes,
        scratch_shapes=[pltpu.VMEM((tm + CARRY_ROWS, CONV_WIDTH), F32)],
        compiler_params=pltpu.CompilerParams(dimension_semantics=("arbitrary", "arbitrary"),
                                             vmem_limit_bytes=VMEM_LIMIT_BYTES),
    )(x, g, win, bg, cw, cb, wa)


def _attn_kernel(q_ref, kp_ref, kc_ref, vp_ref, vc_ref, o_ref, z_ref, *, units, chunks_dil1):
    Q = ATTN_BLOCK
    g = pl.program_id(0)
    c = pl.program_id(2)
    chunks_per_stream = lax.shift_right_logical(jnp.int32(chunks_dil1), 2 * g)
    stream_start = jnp.bitwise_and(c, chunks_per_stream - 1) == 0

    rows = lax.broadcasted_iota(jnp.int32, (HEADS_PER_GROUP * Q, 2 * Q), 0)
    cols = lax.broadcasted_iota(jnp.int32, (HEADS_PER_GROUP * Q, 2 * Q), 1)
    qi = jnp.bitwise_and(rows, Q - 1)
    band = (cols >= qi) & (cols <= qi + Q)
    band_first = band & (cols >= jnp.where(stream_start, Q, 0))
    lane_head = lax.broadcasted_iota(jnp.int32, (Q, GROUP_WIDTH), 1) // HEAD_DIM

    for u in range(units):
        r0 = u * Q
        q_u = q_ref[0, 0, r0:r0 + Q, :]
        if u == 0:
            k_prev, v_prev = kp_ref[0, 0], vp_ref[0, 0]
        else:
            k_prev, v_prev = kc_ref[0, 0, r0 - Q:r0, :], vc_ref[0, 0, r0 - Q:r0, :]
        k_cat = jnp.concatenate([k_prev, kc_ref[0, 0, r0:r0 + Q, :]], axis=0)
        v_cat = jnp.concatenate([v_prev, vc_ref[0, 0, r0:r0 + Q, :]], axis=0)
        q_st = jnp.concatenate(
            [jnp.where(lane_head == hh, q_u, jnp.zeros_like(q_u)) for hh in range(HEADS_PER_GROUP)], axis=0)
        s = lax.dot_general(q_st, k_cat, (((1,), (1,)), ((), ())), preferred_element_type=F32)
        s = jnp.where(band_first if u == 0 else band, s, NEG_INF)
        m = jnp.max(s, axis=-1, keepdims=True)
        p = jnp.exp(s - m)
        l = jnp.sum(p, axis=-1, keepdims=True)
        pv = jnp.dot(p.astype(BF16), v_cat, preferred_element_type=F32)
        zz = m + jnp.log(l)
        o_u = jnp.zeros((Q, GROUP_WIDTH), F32)
        z_u = jnp.zeros((Q, GROUP_WIDTH), F32)
        for hh in range(HEADS_PER_GROUP):
            sel = lane_head == hh
            o_u = jnp.where(sel, pv[hh * Q:(hh + 1) * Q, :] / l[hh * Q:(hh + 1) * Q, :], o_u)
            z_u = jnp.where(sel, zz[hh * Q:(hh + 1) * Q, :], z_u)
        o_ref[0, 0, r0:r0 + Q, :] = o_u.astype(o_ref.dtype)
        z_ref[0, 0, r0:r0 + Q, :] = z_u


def _attention(qs, ks, vs, *, units):
    G, B, S, W = qs.shape
    Q = ATTN_BLOCK
    ch = units * Q
    n_chunks = S // ch
    cur = pl.BlockSpec((1, 1, ch, W), lambda g, b, c: (g, b, c, 0))
    prev = pl.BlockSpec((1, 1, Q, W), lambda g, b, c: (g, b, jnp.maximum(c * units - 1, 0), 0))
    return pl.pallas_call(
        functools.partial(_attn_kernel, units=units, chunks_dil1=n_chunks),
        grid=(G, B, n_chunks),
        in_specs=[cur, prev, cur, prev, cur],
        out_specs=[cur, cur],
        out_shape=[jax.ShapeDtypeStruct((G, B, S, W), BF16), jax.ShapeDtypeStruct((G, B, S, W), F32)],
        compiler_params=pltpu.CompilerParams(dimension_semantics=("arbitrary", "arbitrary", "arbitrary"),
                                             vmem_limit_bytes=VMEM_LIMIT_BYTES),
    )(qs, ks, ks, vs, vs)


def _mix_out_ffn_kernel(x_ref, o_ref, z_ref, ga_ref, sb_ref, wb_ref, wo_ref, g2_ref, wup_ref,
                        fw_ref, fb_ref, wdn_ref, g3_ref, out_ref, up_scr, *, tm, ff_chunk):
    t = pl.program_id(1)

    z = [z_ref[gi, 0] for gi in range(N_GROUPS)]
    z_max = jnp.maximum(jnp.maximum(z[0], z[1]), z[2])
    e = [jnp.exp(zg - z_max) for zg in z]
    den = e[0] + e[1] + e[2]
    yb = jnp.concatenate([(e[gi] / den * o_ref[gi, 0].astype(F32)).astype(BF16) for gi in range(N_GROUPS)],
                         axis=-1)
    pb = jnp.dot(yb, wb_ref[...], preferred_element_type=F32)
    merged = (ga_ref[0].astype(F32) + sb_ref[0].astype(F32) * pb).astype(BF16)
    x1 = x_ref[0] + jnp.dot(merged, wo_ref[...], preferred_element_type=F32)

    h2 = _rms_norm(x1, g2_ref[...]).astype(BF16)

    @pl.when(t == 0)
    def _():
        up_scr[0:CARRY_ROWS, :] = jnp.zeros((CARRY_ROWS, 2 * D_FF), F32)

    acc = x1
    for c0 in range(0, D_FF, ff_chunk):
        gcols = slice(c0, c0 + ff_chunk)
        vcols = slice(D_FF + c0, D_FF + c0 + ff_chunk)
        up_g = jnp.dot(h2, wup_ref[:, gcols], preferred_element_type=F32)
        up_v = jnp.dot(h2, wup_ref[:, vcols], preferred_element_type=F32)
        up_scr[CARRY_ROWS:CARRY_ROWS + tm, gcols] = up_g
        up_scr[CARRY_ROWS:CARRY_ROWS + tm, vcols] = up_v
        gate = _causal_conv3(up_scr, up_g, fw_ref, fb_ref, gcols, tm)
        val = _causal_conv3(up_scr, up_v, fw_ref, fb_ref, vcols, tm)
        act = (gate * jax.nn.sigmoid(gate) * val).astype(BF16)
        acc = acc + jnp.dot(act, wdn_ref[gcols, :], preferred_element_type=F32)
    up_scr[0:CARRY_ROWS, :] = up_scr[tm:tm + CARRY_ROWS, :]

    out_ref[0] = _rms_norm(acc, g3_ref[...])


def _mix_out_ffn(x, o, z, ga, sb, wb, wo, g2, wup, fw, fb, wdn, g3, *, tm, ff_chunk):
    B, S, D = x.shape
    row = lambda w: pl.BlockSpec((1, tm, w), lambda b, t: (b, t, 0))
    grp = pl.BlockSpec((N_GROUPS, 1, tm, GROUP_WIDTH), lambda b, t: (0, b, t, 0))
    return pl.pallas_call(
        functools.partial(_mix_out_ffn_kernel, tm=tm, ff_chunk=ff_chunk),
        grid=(B, S // tm),
        in_specs=[row(D), grp, grp, row(D), row(D),
                  _const_spec((ATTN_WIDTH, D)), _const_spec((D, D)), _const_spec((1, D)),
                  _const_spec((D, 2 * D_FF)), _const_spec((FFN_K, 2 * D_FF)), _const_spec((1, 2 * D_FF)),
                  _const_spec((D_FF, D)), _const_spec((1, D))],
        out_specs=row(D),
        out_shape=jax.ShapeDtypeStruct((B, S, D), F32),
        scratch_shapes=[pltpu.VMEM((tm + CARRY_ROWS, 2 * D_FF), F32)],
        compiler_params=pltpu.CompilerParams(dimension_semantics=("arbitrary", "arbitrary"),
                                             vmem_limit_bytes=VMEM_LIMIT_BYTES),
    )(x, o, z, ga, sb, wb, wo, g2, wup, fw, fb, wdn, g3)


def _to_streams(t, dilation):
    B, S, W = t.shape
    return t.reshape(B, S // dilation, dilation, W).transpose(0, 2, 1, 3).reshape(B, S, W)


def _from_streams(t, dilation):
    B, S, W = t.shape
    return t.reshape(B, dilation, S // dilation, W).transpose(0, 2, 1, 3).reshape(B, S, W)


def kernel(x, norm_mix_g, w_in, b_gate, conv_a_w, conv_a_b, w_proj_a, w_proj_b, w_out, norm_ffn_g, w_up,
           ffn_conv_w, ffn_conv_b, w_down, final_norm_g):
    B, S, D = x.shape
    assert D == D_MODEL and norm_mix_g.shape[0] == 1
    assert all(w // d == ATTN_BLOCK and d == 4 ** gi for gi, (w, d) in enumerate(ATTN_GROUPS))
    max_dil = ATTN_GROUPS[-1][1]
    units = min(4, S // (max_dil * ATTN_BLOCK))
    assert units >= 1 and S % (max_dil * ATTN_BLOCK * units) == 0
    tm1 = min(512, S)
    tm3 = min(256, S)

    q, k, v, ga, sb = _mixer_in(
        x, norm_mix_g[0][None, :], w_in[0].astype(BF16), b_gate[0], conv_a_w[0], conv_a_b[0][None, :],
        w_proj_a[0].astype(BF16), tm=tm1)

    def streams(t):
        return jnp.stack([_to_streams(t[:, :, gi * GROUP_WIDTH:(gi + 1) * GROUP_WIDTH], d)
                          for gi, (_, d) in enumerate(ATTN_GROUPS)])

    o_s, z_s = _attention(streams(q), streams(k), streams(v), units=units)
    o = jnp.stack([_from_streams(o_s[gi], d) for gi, (_, d) in enumerate(ATTN_GROUPS)])
    z = jnp.stack([_from_streams(z_s[gi], d) for gi, (_, d) in enumerate(ATTN_GROUPS)])

    return _mix_out_ffn(
        x, o, z, ga, sb, w_proj_b[0].astype(BF16), w_out[0].astype(BF16), norm_ffn_g[0][None, :],
        w_up[0].astype(BF16), ffn_conv_w[0], ffn_conv_b[0][None, :], w_down[0].astype(BF16),
        final_norm_g[None, :], tm=tm3, ff_chunk=D_FF // 2)
```

```python
import functools

import jax
import jax.numpy as jnp
from jax import lax
from jax.experimental import pallas as pl
from jax.experimental.pallas import tpu as pltpu

D_MODEL = 1024
CONV_WIDTH = 512
CONV_K = 3
ATTN_GROUPS = ((128, 1), (512, 4), (2048, 16))
N_GROUPS = 3
HEADS_PER_GROUP = 4
HEAD_DIM = 64
GROUP_WIDTH = HEADS_PER_GROUP * HEAD_DIM
QKV_WIDTH = 3 * GROUP_WIDTH
ATTN_WIDTH = N_GROUPS * GROUP_WIDTH
ATTN_BLOCK = 128
D_FF = 2816
FFN_K = 3
EPS = 1e-6
NEG_INF = -1e30

OFF_AB, OFF_AC, OFF_AV = 0, CONV_WIDTH, 2 * CONV_WIDTH
OFF_QKV = 3 * CONV_WIDTH
OFF_GA = OFF_QKV + 3 * ATTN_WIDTH
OFF_GB = OFF_GA + D_MODEL
D_IN = OFF_GB + D_MODEL

LANES = 128
SLABS = GROUP_WIDTH // LANES
CARRY_ROWS = 8
VMEM_LIMIT_BYTES = 56 * 1024 * 1024

BF16 = jnp.bfloat16
F32 = jnp.float32


def _const_spec(shape):
    return pl.BlockSpec(shape, lambda *_: (0,) * len(shape), pipeline_mode=pl.Buffered(1))


def _rms_norm(x, g):
    return x * lax.rsqrt(jnp.mean(x * x, axis=-1, keepdims=True) + EPS) * g


def _causal_conv3(scr, cur, w_ref, b_ref, cols, tm):
    c0 = CARRY_ROWS
    return (w_ref[0:1, cols] * scr[c0 - 2:c0 - 2 + tm, cols]
            + w_ref[1:2, cols] * scr[c0 - 1:c0 - 1 + tm, cols]
            + w_ref[2:3, cols] * cur + b_ref[0:1, cols])


def _mixer_in_kernel(x_ref, g_ref, win_ref, bg_ref, cw_ref, cb_ref, wa_ref,
                     s0_ref, s1_ref, s2_ref, ga_ref, sb_ref, u_scr, perm_scr, *, tm):
    t = pl.program_id(1)
    h = _rms_norm(x_ref[0], g_ref[...]).astype(BF16)

    @pl.when(t == 0)
    def _():
        u_scr[0:CARRY_ROWS, :] = jnp.zeros((CARRY_ROWS, CONV_WIDTH), F32)

    p1 = jnp.dot(h, win_ref[:, OFF_AB:OFF_QKV], preferred_element_type=F32)
    u = p1[:, OFF_AC:OFF_AV] * p1[:, OFF_AV:OFF_QKV]
    u_scr[CARRY_ROWS:CARRY_ROWS + tm, :] = u
    conv = _causal_conv3(u_scr, u, cw_ref, cb_ref, slice(0, CONV_WIDTH), tm)
    u_scr[0:CARRY_ROWS, :] = u_scr[tm:tm + CARRY_ROWS, :]
    ya = (p1[:, OFF_AB:OFF_AC] * conv).astype(BF16)
    pa = jnp.dot(ya, wa_ref[...], preferred_element_type=F32)

    g_a = jnp.dot(h, win_ref[:, OFF_GA:OFF_GB], preferred_element_type=F32)
    ga_ref[0] = (jax.nn.sigmoid(g_a + bg_ref[0:1, :]) * pa).astype(BF16)
    g_b = jnp.dot(h, win_ref[:, OFF_GB:D_IN], preferred_element_type=F32)
    sb_ref[0] = jax.nn.sigmoid(g_b + bg_ref[1:2, :]).astype(BF16)

    qkv = jnp.dot(h, win_ref[:, OFF_QKV:OFF_GA], preferred_element_type=F32)
    out_refs = (s0_ref, s1_ref, s2_ref)
    for gi, (_, dil) in enumerate(ATTN_GROUPS):
        res = qkv[:, gi * QKV_WIDTH:(gi + 1) * QKV_WIDTH]
        o_ref = out_refs[gi]
        for j in range(QKV_WIDTH // LANES):
            col = res[:, j * LANES:(j + 1) * LANES]
            if j < SLABS:
                col = col * (HEAD_DIM ** -0.5)
            cols = slice(j * LANES, (j + 1) * LANES)
            if dil == 1:
                for blk in range(tm // ATTN_BLOCK):
                    o_ref[0, blk, 0, :, cols] = col[blk * ATTN_BLOCK:(blk + 1) * ATTN_BLOCK].astype(BF16)
            else:
                perm_scr[j] = col
                rows = tm // dil
                for r in range(dil):
                    stream = perm_scr[j, pl.ds(r, rows, stride=dil), :].astype(BF16)
                    if rows >= ATTN_BLOCK:
                        for blk in range(rows // ATTN_BLOCK):
                            o_ref[0, blk, r, :, cols] = stream[blk * ATTN_BLOCK:(blk + 1) * ATTN_BLOCK]
                    else:
                        o_ref[0, 0, r, :, cols] = stream


def _stream_shape(B, S, dil):
    return (B, S // (ATTN_BLOCK * dil), dil, ATTN_BLOCK, QKV_WIDTH)


def _stream_out_spec(tm, dil):
    span = ATTN_BLOCK * dil
    if tm >= span:
        n = tm // span
        return pl.BlockSpec((1, n, dil, ATTN_BLOCK, QKV_WIDTH), lambda b, t: (b, t, 0, 0, 0))
    per = span // tm
    return pl.BlockSpec((1, 1, dil, tm // dil, QKV_WIDTH), lambda b, t: (b, t // per, 0, t % per, 0))


def _mixer_in(x, g, win, bg, cw, cb, wa, *, tm):
    B, S, D = x.shape
    row = lambda w: pl.BlockSpec((1, tm, w), lambda b, t: (b, t, 0))
    out_shapes = ([jax.ShapeDtypeStruct(_stream_shape(B, S, d), BF16) for _, d in ATTN_GROUPS]
                  + [jax.ShapeDtypeStruct((B, S, D), BF16)] * 2)
    return pl.pallas_call(
        functools.partial(_mixer_in_kernel, tm=tm),
        grid=(B, S // tm),
        in_specs=[row(D), _const_spec((1, D)), _const_spec((D, D_IN)), _const_spec((2, D)),
                  _const_spec((CONV_K, CONV_WIDTH)), _const_spec((1, CONV_WIDTH)), _const_spec((CONV_WIDTH, D))],
        out_specs=[_stream_out_spec(tm, d) for _, d in ATTN_GROUPS] + [row(D)] * 2,
        out_shape=out_shapes,
        scratch_shapes=[pltpu.VMEM((tm + CARRY_ROWS, CONV_WIDTH), F32),
                        pltpu.VMEM((QKV_WIDTH // LANES, tm, LANES), F32)],
        compiler_params=pltpu.CompilerParams(dimension_semantics=("arbitrary", "arbitrary"),
                                             vmem_limit_bytes=VMEM_LIMIT_BYTES),
    )(x, g, win, bg, cw, cb, wa)


def _attn_kernel(cur_ref, kp_ref, vp_ref, o_ref, z_ref, *, ns, dil):
    Q = ATTN_BLOCK
    first_step = pl.program_id(1) == 0

    rows = lax.broadcasted_iota(jnp.int32, (HEADS_PER_GROUP * Q, 2 * Q), 0)
    cols = lax.broadcasted_iota(jnp.int32, (HEADS_PER_GROUP * Q, 2 * Q), 1)
    qi = jnp.bitwise_and(rows, Q - 1)
    band = (cols >= qi) & (cols <= qi + Q)
    band_first = band & (cols >= jnp.where(first_step, Q, 0))
    lane_head = lax.broadcasted_iota(jnp.int32, (Q, GROUP_WIDTH), 1) // HEAD_DIM
    kcols = slice(GROUP_WIDTH, 2 * GROUP_WIDTH)
    vcols = slice(2 * GROUP_WIDTH, 3 * GROUP_WIDTH)

    for s in range(ns):
        for r in range(dil):
            q_u = cur_ref[0, s, r, :, 0:GROUP_WIDTH]
            if s == 0:
                k_prev, v_prev = kp_ref[0, 0, r], vp_ref[0, 0, r]
            else:
                k_prev, v_prev = cur_ref[0, s - 1, r, :, kcols], cur_ref[0, s - 1, r, :, vcols]
            k_cat = jnp.concatenate([k_prev, cur_ref[0, s, r, :, kcols]], axis=0)
            v_cat = jnp.concatenate([v_prev, cur_ref[0, s, r, :, vcols]], axis=0)
            q_st = jnp.concatenate(
                [jnp.where(lane_head == hh, q_u, jnp.zeros_like(q_u)) for hh in range(HEADS_PER_GROUP)], axis=0)
            sc = lax.dot_general(q_st, k_cat, (((1,), (1,)), ((), ())), preferred_element_type=F32)
            sc = jnp.where(band_first if s == 0 else band, sc, NEG_INF)
            m = jnp.max(sc, axis=-1, keepdims=True)
            p = jnp.exp(sc - m)
            l = jnp.sum(p, axis=-1, keepdims=True)
            pv = jnp.dot(p.astype(BF16), v_cat, preferred_element_type=F32)
            zz = m + jnp.log(l)
            o_u = jnp.zeros((Q, GROUP_WIDTH), F32)
            z_u = jnp.zeros((Q, GROUP_WIDTH), F32)
            for hh in range(HEADS_PER_GROUP):
                sel = lane_head == hh
                o_u = jnp.where(sel, pv[hh * Q:(hh + 1) * Q, :] / l[hh * Q:(hh + 1) * Q, :], o_u)
                z_u = jnp.where(sel, zz[hh * Q:(hh + 1) * Q, :], z_u)
            start = s * Q * dil + r
            tok = pl.ds(start, Q) if dil == 1 else pl.ds(start, Q, stride=dil)
            for j in range(SLABS):
                o_ref[0, j, tok, :] = o_u[:, j * LANES:(j + 1) * LANES]
                z_ref[0, j, tok, :] = z_u[:, j * LANES:(j + 1) * LANES]


def _attention(qkv_s, *, dil, units_per_step):
    B, n_spans, _, Q, W = qkv_s.shape
    S = n_spans * dil * Q
    ns = max(units_per_step // dil, 1)
    assert n_spans % ns == 0
    tokens = ns * dil * Q
    cur = pl.BlockSpec((1, ns, dil, Q, W), lambda b, n: (b, n, 0, 0, 0))
    prev = lambda cb: pl.BlockSpec((1, 1, dil, Q, GROUP_WIDTH),
                                   lambda b, n: (b, jnp.maximum(n * ns - 1, 0), 0, 0, cb))
    out = pl.BlockSpec((1, SLABS, tokens, LANES), lambda b, n: (b, 0, n, 0))
    return pl.pallas_call(
        functools.partial(_attn_kernel, ns=ns, dil=dil),
        grid=(B, n_spans // ns),
        in_specs=[cur, prev(1), prev(2)],
        out_specs=[out, out],
        out_shape=[jax.ShapeDtypeStruct((B, SLABS, S, LANES), F32)] * 2,
        compiler_params=pltpu.CompilerParams(dimension_semantics=("arbitrary", "arbitrary"),
                                             vmem_limit_bytes=VMEM_LIMIT_BYTES),
    )(qkv_s, qkv_s, qkv_s)


def _mix_out_ffn_kernel(x_ref, o0_ref, o1_ref, o2_ref, z0_ref, z1_ref, z2_ref, ga_ref, sb_ref, wb_ref, wo_ref,
                        g2_ref, wup_ref, fw_ref, fb_ref, wdn_ref, g3_ref, out_ref, up_scr, *, tm, ff_chunk):
    t = pl.program_id(1)
    wide = lambda ref: jnp.concatenate([ref[0, j] for j in range(SLABS)], axis=-1)

    z = [wide(ref) for ref in (z0_ref, z1_ref, z2_ref)]
    z_max = jnp.maximum(jnp.maximum(z[0], z[1]), z[2])
    e = [jnp.exp(zg - z_max) for zg in z]
    den = e[0] + e[1] + e[2]
    yb = jnp.concatenate([(e[gi] / den * wide(ref)).astype(BF16)
                          for gi, ref in enumerate((o0_ref, o1_ref, o2_ref))], axis=-1)
    pb = jnp.dot(yb, wb_ref[...], preferred_element_type=F32)
    merged = (ga_ref[0].astype(F32) + sb_ref[0].astype(F32) * pb).astype(BF16)
    x1 = x_ref[0] + jnp.dot(merged, wo_ref[...], preferred_element_type=F32)

    h2 = _rms_norm(x1, g2_ref[...]).astype(BF16)

    @pl.when(t == 0)
    def _():
        up_scr[0:CARRY_ROWS, :] = jnp.zeros((CARRY_ROWS, 2 * D_FF), F32)

    acc = x1
    for c0 in range(0, D_FF, ff_chunk):
        gcols = slice(c0, c0 + ff_chunk)
        vcols = slice(D_FF + c0, D_FF + c0 + ff_chunk)
        up_g = jnp.dot(h2, wup_ref[:, gcols], preferred_element_type=F32)
        up_v = jnp.dot(h2, wup_ref[:, vcols], preferred_element_type=F32)
        up_scr[CARRY_ROWS:CARRY_ROWS + tm, gcols] = up_g
        up_scr[CARRY_ROWS:CARRY_ROWS + tm, vcols] = up_v
        gate = _causal_conv3(up_scr, up_g, fw_ref, fb_ref, gcols, tm)
        val = _causal_conv3(up_scr, up_v, fw_ref, fb_ref, vcols, tm)
        act = (gate * jax.nn.sigmoid(gate) * val).astype(BF16)
        acc = acc + jnp.dot(act, wdn_ref[gcols, :], preferred_element_type=F32)
    up_scr[0:CARRY_ROWS, :] = up_scr[tm:tm + CARRY_ROWS, :]

    out_ref[0] = _rms_norm(acc, g3_ref[...])


def _mix_out_ffn(x, o, z, ga, sb, wb, wo, g2, wup, fw, fb, wdn, g3, *, tm, ff_chunk):
    B, S, D = x.shape
    row = lambda w: pl.BlockSpec((1, tm, w), lambda b, t: (b, t, 0))
    slab = pl.BlockSpec((1, SLABS, tm, LANES), lambda b, t: (b, 0, t, 0))
    return pl.pallas_call(
        functools.partial(_mix_out_ffn_kernel, tm=tm, ff_chunk=ff_chunk),
        grid=(B, S // tm),
        in_specs=[row(D)] + [slab] * (2 * N_GROUPS) + [row(D), row(D),
                  _const_spec((ATTN_WIDTH, D)), _const_spec((D, D)), _const_spec((1, D)),
                  _const_spec((D, 2 * D_FF)), _const_spec((FFN_K, 2 * D_FF)), _const_spec((1, 2 * D_FF)),
                  _const_spec((D_FF, D)), _const_spec((1, D))],
        out_specs=row(D),
        out_shape=jax.ShapeDtypeStruct((B, S, D), F32),
        scratch_shapes=[pltpu.VMEM((tm + CARRY_ROWS, 2 * D_FF), F32)],
        compiler_params=pltpu.CompilerParams(dimension_semantics=("arbitrary", "arbitrary"),
                                             vmem_limit_bytes=VMEM_LIMIT_BYTES),
    )(x, *o, *z, ga, sb, wb, wo, g2, wup, fw, fb, wdn, g3)


def _prep_w_in(w_in):
    qkv = w_in[:, OFF_QKV:OFF_GA].reshape(D_MODEL, 3, N_GROUPS, GROUP_WIDTH)
    qkv = qkv.transpose(0, 2, 1, 3).reshape(D_MODEL, 3 * ATTN_WIDTH)
    return jnp.concatenate([w_in[:, :OFF_QKV], qkv, w_in[:, OFF_GA:]], axis=1).astype(BF16)


def kernel(x, norm_mix_g, w_in, b_gate, conv_a_w, conv_a_b, w_proj_a, w_proj_b, w_out, norm_ffn_g, w_up,
           ffn_conv_w, ffn_conv_b, w_down, final_norm_g):
    B, S, D = x.shape
    assert D == D_MODEL and norm_mix_g.shape[0] == 1
    assert all(w // d == ATTN_BLOCK and d == 4 ** gi for gi, (w, d) in enumerate(ATTN_GROUPS))
    max_span = ATTN_GROUPS[-1][1] * ATTN_BLOCK
    assert S % max_span == 0
    tm1 = 512
    tm3 = 256

    *qkv_s, ga, sb = _mixer_in(
        x, norm_mix_g[0][None, :], _prep_w_in(w_in[0]), b_gate[0], conv_a_w[0], conv_a_b[0][None, :],
        w_proj_a[0].astype(BF16), tm=tm1)

    o, z = zip(*[_attention(qkv_s[gi], dil=d, units_per_step=max_span // ATTN_BLOCK)
                 for gi, (_, d) in enumerate(ATTN_GROUPS)])

    return _mix_out_ffn(
        x, o, z, ga, sb, w_proj_b[0].astype(BF16), w_out[0].astype(BF16), norm_ffn_g[0][None, :],
        w_up[0].astype(BF16), ffn_conv_w[0], ffn_conv_b[0][None, :], w_down[0].astype(BF16),
        final_norm_g[None, :], tm=tm3, ff_chunk=D_FF // 2)
```

```python
import functools

import jax
import jax.numpy as jnp
from jax import lax
from jax.experimental import pallas as pl
from jax.experimental.pallas import tpu as pltpu

D_MODEL = 1024
CONV_WIDTH = 512
CONV_K = 3
ATTN_GROUPS = ((128, 1), (512, 4), (2048, 16))
N_GROUPS = 3
HEADS_PER_GROUP = 4
HEAD_DIM = 64
GROUP_WIDTH = HEADS_PER_GROUP * HEAD_DIM
QKV_WIDTH = 3 * GROUP_WIDTH
ATTN_WIDTH = N_GROUPS * GROUP_WIDTH
ATTN_BLOCK = 128
D_FF = 2816
FF_CHUNK = 256
FFN_K = 3
EPS = 1e-6
NEG_INF = -1e30

OFF_AB, OFF_AC, OFF_AV = 0, CONV_WIDTH, 2 * CONV_WIDTH
OFF_QKV = 3 * CONV_WIDTH
OFF_GA = OFF_QKV + 3 * ATTN_WIDTH
OFF_GB = OFF_GA + D_MODEL
D_IN = OFF_GB + D_MODEL

LANES = 128
SLABS = GROUP_WIDTH // LANES
CARRY_ROWS = 8
VMEM_LIMIT_BYTES = 56 * 1024 * 1024

BF16 = jnp.bfloat16
F32 = jnp.float32


def _const_spec(shape):
    return pl.BlockSpec(shape, lambda *_: (0,) * len(shape), pipeline_mode=pl.Buffered(1))


def _rms_norm(x, g):
    return x * lax.rsqrt(jnp.mean(x * x, axis=-1, keepdims=True) + EPS) * g


def _conv_scratch(n_slabs, tm):
    return pltpu.VMEM((n_slabs, tm + CARRY_ROWS, LANES), F32)


def _stage_conv_input(scr, val, tm):
    for j in range(scr.shape[0]):
        scr[j, CARRY_ROWS:CARRY_ROWS + tm, :] = val[:, j * LANES:(j + 1) * LANES]


def _carry_conv_tail(scr, tm):
    scr[:, 0:CARRY_ROWS, :] = scr[:, tm:tm + CARRY_ROWS, :]


def _causal_conv3_pair(scr, cur, w_ref, b_ref, slab, pair_stride, col0, tm):
    c0 = CARRY_ROWS
    sh2 = scr[pl.ds(slab, 2, stride=pair_stride), pl.ds(c0 - 2, tm), :]
    sh1 = scr[pl.ds(slab, 2, stride=pair_stride), pl.ds(c0 - 1, tm), :]
    outs = []
    for i, sl in enumerate((slab, slab + pair_stride)):
        cols = slice(col0 + sl * LANES, col0 + (sl + 1) * LANES)
        outs.append(w_ref[0:1, cols] * sh2[i] + w_ref[1:2, cols] * sh1[i]
                    + w_ref[2:3, cols] * cur[:, sl * LANES:(sl + 1) * LANES] + b_ref[0:1, cols])
    return outs


def _mixer_in_kernel(x_ref, g_ref, win_ref, bg_ref, cw_ref, cb_ref, wa_ref,
                     s0_ref, s1_ref, s2_ref, ga_ref, sb_ref, u_scr, perm_scr, *, tm):
    t = pl.program_id(1)
    h = _rms_norm(x_ref[0], g_ref[...]).astype(BF16)

    @pl.when(t == 0)
    def _():
        u_scr[:, 0:CARRY_ROWS, :] = jnp.zeros((CONV_WIDTH // LANES, CARRY_ROWS, LANES), F32)

    p1 = jnp.dot(h, win_ref[:, OFF_AB:OFF_QKV], preferred_element_type=F32)
    u = p1[:, OFF_AC:OFF_AV] * p1[:, OFF_AV:OFF_QKV]
    _stage_conv_input(u_scr, u, tm)
    c0, c2 = _causal_conv3_pair(u_scr, u, cw_ref, cb_ref, 0, 2, 0, tm)
    c1, c3 = _causal_conv3_pair(u_scr, u, cw_ref, cb_ref, 1, 2, 0, tm)
    conv = jnp.concatenate([c0, c1, c2, c3], axis=-1)
    _carry_conv_tail(u_scr, tm)
    ya =(p1[:, OFF_AB:OFF_AC] * conv).astype(BF16)
    pa = jnp.dot(ya, wa_ref[...], preferred_element_type=F32)

    g_a = jnp.dot(h, win_ref[:, OFF_GA:OFF_GB], preferred_element_type=F32)
    ga_ref[0] = (jax.nn.sigmoid(g_a + bg_ref[0:1, :]) * pa).astype(BF16)
    g_b = jnp.dot(h, win_ref[:, OFF_GB:D_IN], preferred_element_type=F32)
    sb_ref[0] = jax.nn.sigmoid(g_b + bg_ref[1:2, :]).astype(BF16)

    qkv = jnp.dot(h, win_ref[:, OFF_QKV:OFF_GA], preferred_element_type=F32)
    out_refs = (s0_ref, s1_ref, s2_ref)
    for gi, (_, dil) in enumerate(ATTN_GROUPS):
        res = qkv[:, gi * QKV_WIDTH:(gi + 1) * QKV_WIDTH]
        o_ref = out_refs[gi]
        for j in range(QKV_WIDTH // LANES):
            col = res[:, j * LANES:(j + 1) * LANES]
            if j < SLABS:
                col = col * (HEAD_DIM ** -0.5)
            cols = slice(j * LANES, (j + 1) * LANES)
            if dil == 1:
                for blk in range(tm // ATTN_BLOCK):
                    o_ref[0, blk, 0, :, cols] = col[blk * ATTN_BLOCK:(blk + 1) * ATTN_BLOCK].astype(BF16)
            else:
                perm_scr[j] = col
                rows = tm // dil
                for r in range(dil):
                    stream = perm_scr[j, pl.ds(r, rows, stride=dil), :].astype(BF16)
                    if rows >= ATTN_BLOCK:
                        for blk in range(rows // ATTN_BLOCK):
                            o_ref[0, blk, r, :, cols] = stream[blk * ATTN_BLOCK:(blk + 1) * ATTN_BLOCK]
                    else:
                        o_ref[0, 0, r, :, cols] = stream


def _stream_shape(B, S, dil):
    return (B, S // (ATTN_BLOCK * dil), dil, ATTN_BLOCK, QKV_WIDTH)


def _stream_out_spec(tm, dil):
    span = ATTN_BLOCK * dil
    if tm >= span:
        n = tm // span
        return pl.BlockSpec((1, n, dil, ATTN_BLOCK, QKV_WIDTH), lambda b, t: (b, t, 0, 0, 0))
    per = span // tm
    return pl.BlockSpec((1, 1, dil, tm // dil, QKV_WIDTH), lambda b, t: (b, t // per, 0, t % per, 0))


def _mixer_in(x, g, win, bg, cw, cb, wa, *, tm):
    B, S, D = x.shape
    row = lambda w: pl.BlockSpec((1, tm, w), lambda b, t: (b, t, 0))
    out_shapes = ([jax.ShapeDtypeStruct(_stream_shape(B, S, d), BF16) for _, d in ATTN_GROUPS]
                  + [jax.ShapeDtypeStruct((B, S, D), BF16)] * 2)
    return pl.pallas_call(
        functools.partial(_mixer_in_kernel, tm=tm),
        grid=(B, S // tm),
        in_specs=[row(D), _const_spec((1, D)), _const_spec((D, D_IN)), _const_spec((2, D)),
                  _const_spec((CONV_K, CONV_WIDTH)), _const_spec((1, CONV_WIDTH)), _const_spec((CONV_WIDTH, D))],
        out_specs=[_stream_out_spec(tm, d) for _, d in ATTN_GROUPS] + [row(D)] * 2,
        out_shape=out_shapes,
        scratch_shapes=[_conv_scratch(CONV_WIDTH // LANES, tm),
                        pltpu.VMEM((QKV_WIDTH // LANES, tm, LANES), F32)],
        compiler_params=pltpu.CompilerParams(dimension_semantics=("arbitrary", "arbitrary"),
                                             vmem_limit_bytes=VMEM_LIMIT_BYTES),
    )(x, g, win, bg, cw, cb, wa)


def _attn_kernel(cur_ref, kp_ref, vp_ref, o_ref, z_ref, *, ns, dil):
    Q = ATTN_BLOCK
    first_step = pl.program_id(1) == 0

    rows = lax.broadcasted_iota(jnp.int32, (HEADS_PER_GROUP * Q, 2 * Q), 0)
    cols = lax.broadcasted_iota(jnp.int32, (HEADS_PER_GROUP * Q, 2 * Q), 1)
    qi = jnp.bitwise_and(rows, Q - 1)
    band = (cols >= qi) & (cols <= qi + Q)
    band_first = band & (cols >= jnp.where(first_step, Q, 0))
    lane_head = lax.broadcasted_iota(jnp.int32, (Q, GROUP_WIDTH), 1) // HEAD_DIM
    kcols = slice(GROUP_WIDTH, 2 * GROUP_WIDTH)
    vcols = slice(2 * GROUP_WIDTH, 3 * GROUP_WIDTH)

    for s in range(ns):
        for r in range(dil):
            q_u = cur_ref[0, s, r, :, 0:GROUP_WIDTH]
            if s == 0:
                k_prev, v_prev = kp_ref[0, 0, r], vp_ref[0, 0, r]
            else:
                k_prev, v_prev = cur_ref[0, s - 1, r, :, kcols], cur_ref[0, s - 1, r, :, vcols]
            k_cat = jnp.concatenate([k_prev, cur_ref[0, s, r, :, kcols]], axis=0)
            v_cat = jnp.concatenate([v_prev, cur_ref[0, s, r, :, vcols]], axis=0)
            q_st = jnp.concatenate(
                [jnp.where(lane_head == hh, q_u, jnp.zeros_like(q_u)) for hh in range(HEADS_PER_GROUP)], axis=0)
            sc = lax.dot_general(q_st, k_cat, (((1,), (1,)), ((), ())), preferred_element_type=F32)
            sc = jnp.where(band_first if s == 0 else band, sc, NEG_INF)
            m = jnp.max(sc, axis=-1, keepdims=True)
            p = jnp.exp(sc - m)
            l = jnp.sum(p, axis=-1, keepdims=True)
            pv = jnp.dot(p.astype(BF16), v_cat, preferred_element_type=F32)
            zz = m + jnp.log(l)
            o_u = jnp.zeros((Q, GROUP_WIDTH), F32)
            z_u = jnp.zeros((Q, GROUP_WIDTH), F32)
            for hh in range(HEADS_PER_GROUP):
                sel = lane_head == hh
                o_u = jnp.where(sel, pv[hh * Q:(hh + 1) * Q, :] / l[hh * Q:(hh + 1) * Q, :], o_u)
                z_u = jnp.where(sel, zz[hh * Q:(hh + 1) * Q, :], z_u)
            start = s * Q * dil + r
            tok = pl.ds(start, Q) if dil == 1 else pl.ds(start, Q, stride=dil)
            for j in range(SLABS):
                o_ref[0, j, tok, :] = o_u[:, j * LANES:(j + 1) * LANES]
                z_ref[0, j, tok, :] = z_u[:, j * LANES:(j + 1) * LANES]


def _attention(qkv_s, *, dil, units_per_step):
    B, n_spans, _, Q, W = qkv_s.shape
    S = n_spans * dil * Q
    ns = max(units_per_step // dil, 1)
    assert n_spans % ns == 0
    tokens = ns * dil * Q
    cur = pl.BlockSpec((1, ns, dil, Q, W), lambda b, n: (b, n, 0, 0, 0))
    prev = lambda cb: pl.BlockSpec((1, 1, dil, Q, GROUP_WIDTH),
                                   lambda b, n: (b, jnp.maximum(n * ns - 1, 0), 0, 0, cb))
    out = pl.BlockSpec((1, SLABS, tokens, LANES), lambda b, n: (b, 0, n, 0))
    return pl.pallas_call(
        functools.partial(_attn_kernel, ns=ns, dil=dil),
        grid=(B, n_spans // ns),
        in_specs=[cur, prev(1), prev(2)],
        out_specs=[out, out],
        out_shape=[jax.ShapeDtypeStruct((B, SLABS, S, LANES), F32)] * 2,
        compiler_params=pltpu.CompilerParams(dimension_semantics=("arbitrary", "arbitrary"),
                                             vmem_limit_bytes=VMEM_LIMIT_BYTES),
    )(qkv_s, qkv_s, qkv_s)


def _mix_out_ffn_kernel(x_ref, o0_ref, o1_ref, o2_ref, z0_ref, z1_ref, z2_ref, ga_ref, sb_ref, wb_ref, wo_ref,
                        g2_ref, wup_ref, fw_ref, fb_ref, wdn_ref, g3_ref, out_ref, *up_scrs, tm):
    t = pl.program_id(1)
    wide = lambda ref: jnp.concatenate([ref[0, j] for j in range(SLABS)], axis=-1)

    z = [wide(ref) for ref in (z0_ref, z1_ref, z2_ref)]
    z_max = jnp.maximum(jnp.maximum(z[0], z[1]), z[2])
    e = [jnp.exp(zg - z_max) for zg in z]
    den = e[0] + e[1] + e[2]
    yb = jnp.concatenate([(e[gi] / den * wide(ref)).astype(BF16)
                          for gi, ref in enumerate((o0_ref, o1_ref, o2_ref))], axis=-1)
    pb = jnp.dot(yb, wb_ref[...], preferred_element_type=F32)
    merged = (ga_ref[0].astype(F32) + sb_ref[0].astype(F32) * pb).astype(BF16)
    x1 = x_ref[0] + jnp.dot(merged, wo_ref[...], preferred_element_type=F32)

    h2 = _rms_norm(x1, g2_ref[...]).astype(BF16)

    @pl.when(t == 0)
    def _():
        for scr in up_scrs:
            scr[:, 0:CARRY_ROWS, :] = jnp.zeros((scr.shape[0], CARRY_ROWS, LANES), F32)

    acts = []
    for c, scr in enumerate(up_scrs):
        col0 = c * 2 * FF_CHUNK
        up = jnp.dot(h2, wup_ref[:, col0:col0 + 2 * FF_CHUNK], preferred_element_type=F32)
        _stage_conv_input(scr, up, tm)
        for j in range(FF_CHUNK // LANES):
            gate, val = _causal_conv3_pair(scr, up, fw_ref, fb_ref, j, FF_CHUNK // LANES, col0, tm)
            acts.append((gate * jax.nn.sigmoid(gate) * val).astype(BF16))
        _carry_conv_tail(scr, tm)
    act = jnp.concatenate(acts, axis=-1)
    x2 = x1 + jnp.dot(act, wdn_ref[...], preferred_element_type=F32)

    out_ref[0] = _rms_norm(x2, g3_ref[...])


def _mix_out_ffn(x, o, z, ga, sb, wb, wo, g2, wup, fw, fb, wdn, g3, *, tm):
    B, S, D = x.shape
    row = lambda w: pl.BlockSpec((1, tm, w), lambda b, t: (b, t, 0))
    slab = pl.BlockSpec((1, SLABS, tm, LANES), lambda b, t: (b, 0, t, 0))
    return pl.pallas_call(
        functools.partial(_mix_out_ffn_kernel, tm=tm),
        grid=(B, S // tm),
        in_specs=[row(D)] + [slab] * (2 * N_GROUPS) + [row(D), row(D),
                  _const_spec((ATTN_WIDTH, D)), _const_spec((D, D)), _const_spec((1, D)),
                  _const_spec((D, 2 * D_FF)), _const_spec((FFN_K, 2 * D_FF)), _const_spec((1, 2 * D_FF)),
                  _const_spec((D_FF, D)), _const_spec((1, D))],
        out_specs=row(D),
        out_shape=jax.ShapeDtypeStruct((B, S, D), F32),
        scratch_shapes=[_conv_scratch(2 * FF_CHUNK // LANES, tm)] * (D_FF // FF_CHUNK),
        compiler_params=pltpu.CompilerParams(dimension_semantics=("arbitrary", "arbitrary"),
                                             vmem_limit_bytes=VMEM_LIMIT_BYTES),
    )(x, *o, *z, ga, sb, wb, wo, g2, wup, fw, fb, wdn, g3)


def _prep_w_in(w_in):
    qkv = w_in[:, OFF_QKV:OFF_GA].reshape(D_MODEL, 3, N_GROUPS, GROUP_WIDTH)
    qkv = qkv.transpose(0, 2, 1, 3).reshape(D_MODEL, 3 * ATTN_WIDTH)
    return jnp.concatenate([w_in[:, :OFF_QKV], qkv, w_in[:, OFF_GA:]], axis=1).astype(BF16)


def _chunk_gate_val(w):
    rows = w.shape[0]
    return w.reshape(rows, 2, D_FF // FF_CHUNK, FF_CHUNK).transpose(0, 2, 1, 3).reshape(rows, 2 * D_FF)


def kernel(x, norm_mix_g, w_in, b_gate, conv_a_w, conv_a_b, w_proj_a, w_proj_b, w_out, norm_ffn_g, w_up,
           ffn_conv_w, ffn_conv_b, w_down, final_norm_g):
    B, S, D = x.shape
    assert D == D_MODEL and norm_mix_g.shape[0] == 1
    assert all(w // d == ATTN_BLOCK and d == 4 ** gi for gi, (w, d) in enumerate(ATTN_GROUPS))
    max_span = ATTN_GROUPS[-1][1] * ATTN_BLOCK
    assert S % max_span == 0
    tm1 = 512
    tm3 = 256

    *qkv_s, ga, sb = _mixer_in(
        x, norm_mix_g[0][None, :], _prep_w_in(w_in[0]), b_gate[0], conv_a_w[0], conv_a_b[0][None, :],
        w_proj_a[0].astype(BF16), tm=tm1)

    o, z = zip(*[_attention(qkv_s[gi], dil=d, units_per_step=max_span // ATTN_BLOCK)
                 for gi, (_, d) in enumerate(ATTN_GROUPS)])

    return _mix_out_ffn(
        x, o, z, ga, sb, w_proj_b[0].astype(BF16), w_out[0].astype(BF16), norm_ffn_g[0][None, :],
        _chunk_gate_val(w_up[0]).astype(BF16), _chunk_gate_val(ffn_conv_w[0]),
        _chunk_gate_val(ffn_conv_b[0][None, :]), w_down[0].astype(BF16), final_norm_g[None, :], tm=tm3)
```
